```python
import math
import jax, jax.numpy as jnp
from jax import lax
import numpy as np

D_MODEL = 1024
BATCH = 8
SEQ = 8192
DEPTH = 4

N_MIXERS = 2
N_SB = (DEPTH + 1) // 2
N_GLA = DEPTH // 2

ALPHA = (2.0 * DEPTH) ** 0.25
BETA = (8.0 * DEPTH) ** -0.25

SB_HEADS = 16
SB_HEAD_DIM = D_MODEL // SB_HEADS
SB_BLOCK = 128

GLA_HEADS = 4
GLA_KEY_DIM = D_MODEL // 2
GLA_VALUE_DIM = D_MODEL
GLA_DK = GLA_KEY_DIM // GLA_HEADS
GLA_DV = GLA_VALUE_DIM // GLA_HEADS
GLA_GATE_RANK = 16
GLA_GATE_TAU = 16.0
GLA_CHUNK = 64
GLA_IN_DIM = 2 * GLA_KEY_DIM + 2 * GLA_VALUE_DIM

D_FF = ((8 * D_MODEL // 3 + 127) // 128) * 128
CONV_W = 3

LN_EPS = 1e-5

kernel_name = "hybrid_stickbreak_gla_convffn_deepnorm"


def layer_norm(x, g, b):
    xf = x.astype(jnp.float32)
    mu = jnp.mean(xf, axis=-1, keepdims=True)
    var = jnp.mean(jnp.square(xf - mu), axis=-1, keepdims=True)
    y = (xf - mu) * lax.rsqrt(var + LN_EPS)
    return (y * g.astype(jnp.float32) + b.astype(jnp.float32)).astype(x.dtype)


def stick_breaking_attention(x, w_qkv, w_o):
    B, S, _ = x.shape
    qkv = (x @ w_qkv).reshape(B, S, 3, SB_HEADS, SB_HEAD_DIM)
    q, k, v = [jnp.transpose(qkv[:, :, i], (0, 2, 1, 3)) for i in range(3)]
    scale = SB_HEAD_DIM ** -0.5
    outs = []
    for blk in range(S // SB_BLOCK):
        q0 = blk * SB_BLOCK
        kv_len = q0 + SB_BLOCK
        q_b = q[:, :, q0:kv_len]
        k_b = k[:, :, :kv_len]
        v_b = v[:, :, :kv_len]
        z = jnp.einsum('bhtd,bhsd->bhts', q_b, k_b).astype(jnp.float32) * scale
        t_idx = q0 + jnp.arange(SB_BLOCK)[:, None]
        s_idx = jnp.arange(kv_len)[None, :]
        mask = s_idx < t_idx
        log_1m_beta = jnp.where(mask, jax.nn.log_sigmoid(-z), 0.0)
        tail = lax.cumsum(log_1m_beta, axis=3, reverse=True) - log_1m_beta
        log_a = jax.nn.log_sigmoid(z) + tail
        a = jnp.where(mask, jnp.exp(log_a), 0.0)
        outs.append(jnp.einsum('bhts,bhsd->bhtd', a, v_b.astype(jnp.float32)))
    o = jnp.concatenate(outs, axis=2)
    o = jnp.transpose(o, (0, 2, 1, 3)).reshape(B, S, SB_HEADS * SB_HEAD_DIM).astype(x.dtype)
    return o @ w_o


def gated_linear_attention(x, w_in, w_a1, w_a2, b_a, norm_g, w_o):
    B, S, _ = x.shape
    N, C = S // GLA_CHUNK, GLA_CHUNK
    proj = x @ w_in
    q = proj[..., :GLA_KEY_DIM]
    k = proj[..., GLA_KEY_DIM:2 * GLA_KEY_DIM]
    v = proj[..., 2 * GLA_KEY_DIM:2 * GLA_KEY_DIM + GLA_VALUE_DIM]
    r = proj[..., 2 * GLA_KEY_DIM + GLA_VALUE_DIM:]
    g = jax.nn.log_sigmoid(((x @ w_a1) @ w_a2 + b_a).astype(jnp.float32)) / GLA_GATE_TAU

    def to_chunks(t, d):
        return jnp.transpose(t.reshape(B, N, C, GLA_HEADS, d), (0, 3, 1, 2, 4))

    q = to_chunks(q.astype(jnp.float32), GLA_DK) * (GLA_DK ** -0.5)
    k = to_chunks(k.astype(jnp.float32), GLA_DK)
    v = to_chunks(v.astype(jnp.float32), GLA_DV)
    g = to_chunks(g, GLA_DK)
    b = jnp.cumsum(g, axis=3)
    b_last = b[:, :, :, -1:, :]

    q_dec = q * jnp.exp(b)
    k_inv = k * jnp.exp(-b)
    attn = jnp.einsum('bhnti,bhnsi->bhnts', q_dec, k_inv)
    causal = jnp.tril(jnp.ones((C, C), dtype=bool))
    attn = jnp.where(causal, attn, 0.0)
    o_intra = jnp.einsum('bhnts,bhnsv->bhntv', attn, v)

    k_dec = k * jnp.exp(b_last - b)
    chunk_kv = jnp.einsum('bhnsi,bhnsv->bhniv', k_dec, v)
    chunk_decay = jnp.exp(b_last[:, :, :, 0, :])

    def step(state, inp):
        decay, kv = inp
        return decay[..., None] * state + kv, state

    init = jnp.zeros((B, GLA_HEADS, GLA_DK, GLA_DV), jnp.float32)
    _, states = lax.scan(step, init, (jnp.moveaxis(chunk_decay, 2, 0), jnp.moveaxis(chunk_kv, 2, 0)))
    o_inter = jnp.einsum('bhnti,nbhiv->bhntv', q_dec, states)

    o = o_intra + o_inter
    o = jnp.transpose(o, (0, 2, 3, 1, 4)).reshape(B, S, GLA_HEADS, GLA_DV)
    o = o * lax.rsqrt(jnp.mean(jnp.square(o), axis=-1, keepdims=True) + LN_EPS) * norm_g.astype(jnp.float32)
    o = o * jax.nn.silu(r.astype(jnp.float32).reshape(B, S, GLA_HEADS, GLA_DV))
    return o.reshape(B, S, GLA_VALUE_DIM).astype(x.dtype) @ w_o


def conv_ffn(x, w_up, conv_w, conv_b, w_down):
    h = x @ w_up
    h = lax.conv_general_dilated(
        h, conv_w.astype(h.dtype)[:, None, :], window_strides=(1,),
        padding=[(CONV_W - 1, 0)], dimension_numbers=('NWC', 'WIO', 'NWC'),
        feature_group_count=2 * D_FF) + conv_b
    u, gte = h[..., :D_FF], h[..., D_FF:]
    return (jax.nn.silu(gte) * u) @ w_down


def setup_inputs(seed: int = 0) -> dict:
    key = jax.random.key(seed)
    ks = jax.random.split(key, 20)
    nrm = jax.random.normal
    f32 = jnp.float32
    x = nrm(ks[0], (BATCH, SEQ, D_MODEL), f32)

    sb_col = jnp.concatenate([jnp.ones((2 * D_MODEL,), f32), jnp.full((D_MODEL,), BETA, f32)])
    sb_w_qkv = nrm(ks[1], (N_SB, D_MODEL, 3 * D_MODEL), f32) * D_MODEL ** -0.5 * sb_col
    sb_w_o = nrm(ks[2], (N_SB, D_MODEL, D_MODEL), f32) * D_MODEL ** -0.5 * BETA

    gla_col = jnp.concatenate([jnp.ones((2 * GLA_KEY_DIM,), f32), jnp.full((GLA_VALUE_DIM,), BETA, f32),
                               jnp.ones((GLA_VALUE_DIM,), f32)])
    gla_w_in = nrm(ks[3], (N_GLA, D_MODEL, GLA_IN_DIM), f32) * D_MODEL ** -0.5 * gla_col
    gla_w_a1 = nrm(ks[4], (N_GLA, D_MODEL, GLA_GATE_RANK), f32) * D_MODEL ** -0.5
    gla_w_a2 = nrm(ks[5], (N_GLA, GLA_GATE_RANK, GLA_KEY_DIM), f32) * GLA_GATE_RANK ** -0.5
    gla_b_a = nrm(ks[6], (N_GLA, GLA_KEY_DIM), f32) * 0.1 + 1.0
    gla_norm_g = 1.0 + 0.01 * nrm(ks[7], (N_GLA, GLA_DV), f32)
    gla_w_o = nrm(ks[8], (N_GLA, GLA_VALUE_DIM, D_MODEL), f32) * GLA_VALUE_DIM ** -0.5 * BETA

    ffn_w_up = nrm(ks[9], (DEPTH, D_MODEL, 2 * D_FF), f32) * D_MODEL ** -0.5 * BETA
    ffn_conv_w = nrm(ks[10], (DEPTH, CONV_W, 2 * D_FF), f32) * CONV_W ** -0.5
    ffn_conv_b = nrm(ks[11], (DEPTH, 2 * D_FF), f32) * 0.01
    ffn_w_down = nrm(ks[12], (DEPTH, D_FF, D_MODEL), f32) * D_FF ** -0.5 * BETA

    ln_mix_g = 1.0 + 0.01 * nrm(ks[13], (DEPTH, D_MODEL), f32)
    ln_mix_b = 0.01 * nrm(ks[14], (DEPTH, D_MODEL), f32)
    ln_ffn_g = 1.0 + 0.01 * nrm(ks[15], (DEPTH, D_MODEL), f32)
    ln_ffn_b = 0.01 * nrm(ks[16], (DEPTH, D_MODEL), f32)
    return {"x": x, "sb_w_qkv": sb_w_qkv, "sb_w_o": sb_w_o,
            "gla_w_in": gla_w_in, "gla_w_a1": gla_w_a1, "gla_w_a2": gla_w_a2, "gla_b_a": gla_b_a,
            "gla_norm_g": gla_norm_g, "gla_w_o": gla_w_o,
            "ffn_w_up": ffn_w_up, "ffn_conv_w": ffn_conv_w, "ffn_conv_b": ffn_conv_b, "ffn_w_down": ffn_w_down,
            "ln_mix_g": ln_mix_g, "ln_mix_b": ln_mix_b, "ln_ffn_g": ln_ffn_g, "ln_ffn_b": ln_ffn_b}


def reference(x, sb_w_qkv, sb_w_o, gla_w_in, gla_w_a1, gla_w_a2, gla_b_a, gla_norm_g, gla_w_o,
              ffn_w_up, ffn_conv_w, ffn_conv_b, ffn_w_down, ln_mix_g, ln_mix_b, ln_ffn_g, ln_ffn_b):
    for i in range(DEPTH):
        j = i // N_MIXERS
        if i % N_MIXERS == 0:
            y = stick_breaking_attention(x, sb_w_qkv[j], sb_w_o[j])
        else:
            y = gated_linear_attention(x, gla_w_in[j], gla_w_a1[j], gla_w_a2[j], gla_b_a[j],
                                       gla_norm_g[j], gla_w_o[j])
        x = layer_norm(ALPHA * x + y, ln_mix_g[i], ln_mix_b[i])
        y = conv_ffn(x, ffn_w_up[i], ffn_conv_w[i], ffn_conv_b[i], ffn_w_down[i])
        x = layer_norm(ALPHA * x + y, ln_ffn_g[i], ln_ffn_b[i])
    return x
```

```python
import functools

import jax
import jax.numpy as jnp
from jax import lax
from jax.experimental import pallas as pl
from jax.experimental.pallas import tpu as pltpu

F32 = jnp.float32
BF16 = jnp.bfloat16

D_MODEL = 1024
DEPTH = 4
N_MIXERS = 2
ALPHA = (2.0 * DEPTH) ** 0.25

SB_HEADS = 16
SB_HEAD_DIM = D_MODEL // SB_HEADS

GLA_HEADS = 4
GLA_KEY_DIM = D_MODEL // 2
GLA_VALUE_DIM = D_MODEL
GLA_DK = GLA_KEY_DIM // GLA_HEADS
GLA_DV = GLA_VALUE_DIM // GLA_HEADS
GLA_GATE_RANK = 16
GLA_GATE_TAU = 16.0
GLA_CHUNK = 64
GLA_IN_DIM = 2 * GLA_KEY_DIM + 2 * GLA_VALUE_DIM

D_FF = ((8 * D_MODEL // 3 + 127) // 128) * 128
CONV_W = 3
LN_EPS = 1e-5

LANES = 128
MXU_DIM = 256
VMEM_LIMIT_BYTES = 56 * 1024 * 1024

ROW_TILE = 512
SB_BLOCK_Q = 256
SB_BLOCK_K = 256
GLA_TOKENS = 512
FFN_ROW_TILE = 256
CONV_HALO = 8


def _params(n_axes):
    return pltpu.CompilerParams(dimension_semantics=("arbitrary",) * n_axes,
                                vmem_limit_bytes=VMEM_LIMIT_BYTES)


def _layer_norm(r, g, b):
    mu = jnp.mean(r, axis=-1, keepdims=True)
    d = r - mu
    var = jnp.mean(d * d, axis=-1, keepdims=True)
    return d * lax.rsqrt(var + LN_EPS) * g + b


def _proj_kernel(x_ref, w_ref, o_ref):
    xb = x_ref[...].astype(BF16)
    n_out = o_ref.shape[1]
    for c in range(n_out // ROW_TILE):
        cs = slice(c * ROW_TILE, (c + 1) * ROW_TILE)
        o_ref[:, cs] = jnp.dot(xb, w_ref[:, cs], preferred_element_type=F32).astype(o_ref.dtype)


def _proj(x, w):
    n, k = x.shape
    m = w.shape[1]
    return pl.pallas_call(
        _proj_kernel,
        grid=(n // ROW_TILE,),
        in_specs=[pl.BlockSpec((ROW_TILE, k), lambda i: (i, 0)),
                  pl.BlockSpec((k, m), lambda i: (0, 0))],
        out_specs=pl.BlockSpec((ROW_TILE, m), lambda i: (i, 0)),
        out_shape=jax.ShapeDtypeStruct((n, m), BF16),
        compiler_params=_params(1),
        name="proj",
    )(x, w)


def _gla_proj_kernel(x_ref, w_ref, wa1_ref, wa2_ref, ba_ref, o_ref, g_ref):
    xb = x_ref[...].astype(BF16)
    n_out = o_ref.shape[1]
    for c in range(n_out // ROW_TILE):
        cs = slice(c * ROW_TILE, (c + 1) * ROW_TILE)
        o_ref[:, cs] = jnp.dot(xb, w_ref[:, cs], preferred_element_type=F32).astype(o_ref.dtype)
    low = jnp.dot(xb, wa1_ref[...], preferred_element_type=F32).astype(BF16)
    pre = jnp.dot(low, wa2_ref[...], preferred_element_type=F32) + ba_ref[...]
    g_ref[...] = (jnp.minimum(pre, 0.0) - jnp.log(1.0 + jnp.exp(-jnp.abs(pre)))) * (1.0 / GLA_GATE_TAU)


def _gla_proj(x, w_in, w_a1, w_a2, b_a):
    n, k = x.shape
    m = w_in.shape[1]
    rank = w_a1.shape[1]
    return pl.pallas_call(
        _gla_proj_kernel,
        grid=(n // ROW_TILE,),
        in_specs=[pl.BlockSpec((ROW_TILE, k), lambda i: (i, 0)),
                  pl.BlockSpec((k, m), lambda i: (0, 0)),
                  pl.BlockSpec((k, rank), lambda i: (0, 0)),
                  pl.BlockSpec((rank, GLA_KEY_DIM), lambda i: (0, 0)),
                  pl.BlockSpec((1, GLA_KEY_DIM), lambda i: (0, 0))],
        out_specs=[pl.BlockSpec((ROW_TILE, m), lambda i: (i, 0)),
                   pl.BlockSpec((ROW_TILE, GLA_KEY_DIM), lambda i: (i, 0))],
        out_shape=[jax.ShapeDtypeStruct((n, m), BF16),
                   jax.ShapeDtypeStruct((n, GLA_KEY_DIM), F32)],
        compiler_params=_params(1),
        name="gla_proj",
    )(x, w_in, w_a1, w_a2, b_a)


def _oproj_ln_kernel(o_ref, x_ref, w_ref, g_ref, b_ref, out_ref):
    y = jnp.dot(o_ref[...], w_ref[...], preferred_element_type=F32)
    out_ref[...] = _layer_norm(ALPHA * x_ref[...] + y, g_ref[...], b_ref[...])


def _oproj_ln(o, x, w, g, b):
    n, k = o.shape
    d = w.shape[1]
    return pl.pallas_call(
        _oproj_ln_kernel,
        grid=(n // ROW_TILE,),
        in_specs=[pl.BlockSpec((ROW_TILE, k), lambda i: (i, 0)),
                  pl.BlockSpec((ROW_TILE, d), lambda i: (i, 0)),
                  pl.BlockSpec((k, d), lambda i: (0, 0)),
                  pl.BlockSpec((1, d), lambda i: (0, 0)),
                  pl.BlockSpec((1, d), lambda i: (0, 0))],
        out_specs=pl.BlockSpec((ROW_TILE, d), lambda i: (i, 0)),
        out_shape=jax.ShapeDtypeStruct((n, d), F32),
        compiler_params=_params(1),
        name="oproj_ln",
    )(o, x, w, g, b)


def _sb_attn_kernel(q_ref, k_ref, v_ref, u_ref, o_ref, acc_ref, carry_ref):
    qi = pl.program_id(2)
    tq, tk = SB_BLOCK_Q, SB_BLOCK_K
    lane = lax.broadcasted_iota(jnp.int32, (1, LANES), 1)
    head_lanes = [lane < SB_HEAD_DIM, lane >= SB_HEAD_DIM]
    q = q_ref[...]
    q_heads = [jnp.where(m, q, jnp.zeros_like(q)) for m in head_lanes]
    u = u_ref[...]
    row = lax.broadcasted_iota(jnp.int32, (tq, tk), 0)
    col = lax.broadcasted_iota(jnp.int32, (tq, tk), 1)
    strictly_causal = col < row

    acc_ref[...] = jnp.zeros_like(acc_ref)
    carry_ref[...] = jnp.zeros_like(carry_ref)

    def key_block(j, diagonal):
        start = pl.multiple_of(j * tk, tk)
        ks = k_ref[pl.ds(start, tk), :]
        vs = v_ref[pl.ds(start, tk), :]
        contrib = jnp.zeros((tq, LANES), F32)
        for h in range(2):
            z = lax.dot_general(q_heads[h], ks, (((1,), (1,)), ((), ())),
                                preferred_element_type=F32)
            lse = jnp.log(1.0 + jnp.exp(-jnp.abs(z)))
            log_1m_beta = -(jnp.maximum(z, 0.0) + lse)
            if diagonal:
                log_1m_beta = jnp.where(strictly_causal, log_1m_beta, 0.0)
            hi = log_1m_beta.astype(BF16)
            lo = (log_1m_beta - hi.astype(F32)).astype(BF16)
            tail = (jnp.dot(hi, u, preferred_element_type=F32)
                    + jnp.dot(lo, u, preferred_element_type=F32))
            carry = carry_ref[h]
            log_a = (jnp.minimum(z, 0.0) - lse) + tail + carry
            a = jnp.exp(log_a)
            if diagonal:
                a = jnp.where(strictly_causal, a, 0.0)
            vh = jnp.where(head_lanes[h], vs, jnp.zeros_like(vs))
            contrib = contrib + jnp.dot(a.astype(BF16), vh, preferred_element_type=F32)
            carry_ref[h] = carry + jnp.sum(log_1m_beta, axis=-1, keepdims=True)
        acc_ref[...] += contrib

    key_block(qi, True)

    def body(i, c):
        key_block(qi - 1 - i, False)
        return c

    lax.fori_loop(0, qi, body, 0)
    o_ref[...] = acc_ref[...].astype(o_ref.dtype)


def _sb_attention(qkv, batch, seq):
    n = qkv.shape[0]
    assert SB_BLOCK_Q == SB_BLOCK_K and seq % SB_BLOCK_Q == 0
    n_q = seq // SB_BLOCK_Q
    pairs = D_MODEL // LANES
    jj = lax.broadcasted_iota(jnp.int32, (SB_BLOCK_K, SB_BLOCK_K), 0)
    ss = lax.broadcasted_iota(jnp.int32, (SB_BLOCK_K, SB_BLOCK_K), 1)
    u = (jj > ss).astype(BF16)
    return pl.pallas_call(
        _sb_attn_kernel,
        grid=(batch, pairs, n_q),
        in_specs=[pl.BlockSpec((SB_BLOCK_Q, LANES), lambda b, p, i: (b * n_q + i, p)),
                  pl.BlockSpec((seq, LANES), lambda b, p, i: (b, pairs + p)),
                  pl.BlockSpec((seq, LANES), lambda b, p, i: (b, 2 * pairs + p)),
                  pl.BlockSpec((SB_BLOCK_K, SB_BLOCK_K), lambda b, p, i: (0, 0))],
        out_specs=pl.BlockSpec((SB_BLOCK_Q, LANES), lambda b, p, i: (b * n_q + i, p)),
        out_shape=jax.ShapeDtypeStruct((n, D_MODEL), BF16),
        scratch_shapes=[pltpu.VMEM((SB_BLOCK_Q, LANES), F32),
                        pltpu.VMEM((2, SB_BLOCK_Q, 1), F32)],
        compiler_params=_params(3),
        name="sb_attn",
    )(qkv, qkv, qkv, u)


def _gla_kernel(q_ref, k_ref, v_ref, r_ref, g_ref, ng_ref, lt_ref, o_ref, state_ref):
    si = pl.program_id(2)
    c_len = GLA_CHUNK

    @pl.when(si == 0)
    def _():
        state_ref[...] = jnp.zeros_like(state_ref)

    lt = lt_ref[...]
    row = lax.broadcasted_iota(jnp.int32, (c_len, c_len), 0)
    col = lax.broadcasted_iota(jnp.int32, (c_len, c_len), 1)
    causal = col <= row
    norm_g = ng_ref[...]
    nt = (((1,), (1,)), ((), ()))
    tn = (((0,), (0,)), ((), ()))
    for c in range(GLA_TOKENS // c_len):
        rows = slice(c * c_len, (c + 1) * c_len)
        q = q_ref[rows, :].astype(F32) * (GLA_DK ** -0.5)
        k = k_ref[rows, :].astype(F32)
        v = v_ref[rows, :]
        g = g_ref[rows, :]
        g_hi = g.astype(BF16)
        g_lo = (g - g_hi.astype(F32)).astype(BF16)
        b = (jnp.dot(lt, g_hi, preferred_element_type=F32)
             + jnp.dot(lt, g_lo, preferred_element_type=F32))
        b_last = b[c_len - 1:c_len, :]
        q_dec = (q * jnp.exp(b)).astype(BF16)
        k_inv = (k * jnp.exp(-b)).astype(BF16)
        k_dec = (k * jnp.exp(b_last - b)).astype(BF16)
        attn = lax.dot_general(q_dec, k_inv, nt, preferred_element_type=F32)
        attn = jnp.where(causal, attn, 0.0).astype(BF16)
        state = state_ref[...]
        o = (jnp.dot(attn, v, preferred_element_type=F32)
             + lax.dot_general(q_dec, state.astype(BF16), nt, preferred_element_type=F32))
        kv = lax.dot_general(v, k_dec, tn, preferred_element_type=F32)
        state_ref[...] = state * jnp.exp(b_last) + kv
        o = o * lax.rsqrt(jnp.mean(o * o, axis=-1, keepdims=True) + LN_EPS) * norm_g
        r = r_ref[rows, :].astype(F32)
        o_ref[rows, :] = (o * (r / (1.0 + jnp.exp(-r)))).astype(o_ref.dtype)


def _gla(proj, g, norm_g, batch, seq):
    n = proj.shape[0]
    steps = seq // GLA_TOKENS
    t = GLA_TOKENS
    k_off = GLA_KEY_DIM // GLA_DK
    v_off = 2 * GLA_KEY_DIM // GLA_DV
    r_off = (2 * GLA_KEY_DIM + GLA_VALUE_DIM) // GLA_DV
    ii = lax.broadcasted_iota(jnp.int32, (GLA_CHUNK, GLA_CHUNK), 0)
    jj = lax.broadcasted_iota(jnp.int32, (GLA_CHUNK, GLA_CHUNK), 1)
    lt = (jj <= ii).astype(BF16)
    return pl.pallas_call(
        _gla_kernel,
        grid=(batch, GLA_HEADS, steps),
        in_specs=[pl.BlockSpec((t, GLA_DK), lambda b, h, s: (b * steps + s, h)),
                  pl.BlockSpec((t, GLA_DK), lambda b, h, s: (b * steps + s, k_off + h)),
                  pl.BlockSpec((t, GLA_DV), lambda b, h, s: (b * steps + s, v_off + h)),
                  pl.BlockSpec((t, GLA_DV), lambda b, h, s: (b * steps + s, r_off + h)),
                  pl.BlockSpec((t, GLA_DK), lambda b, h, s: (b * steps + s, h)),
                  pl.BlockSpec((1, GLA_DV), lambda b, h, s: (0, 0)),
                  pl.BlockSpec((GLA_CHUNK, GLA_CHUNK), lambda b, h, s: (0, 0))],
        out_specs=pl.BlockSpec((t, GLA_DV), lambda b, h, s: (b * steps + s, h)),
        out_shape=jax.ShapeDtypeStruct((n, GLA_VALUE_DIM), BF16),
        scratch_shapes=[pltpu.VMEM((GLA_DV, GLA_DK), F32)],
        compiler_params=_params(3),
        name="gla",
    )(proj, proj, proj, proj, g, norm_g, lt)


def _ffn_kernel(x_ref, wu_ref, cw_ref, cb_ref, wd_ref, g_ref, b_ref, out_ref, prev_ref, *, tiles_per_seq):
    i = pl.program_id(0)
    tm = FFN_ROW_TILE

    @pl.when(i % tiles_per_seq == 0)
    def _():
        prev_ref[...] = jnp.zeros_like(prev_ref)

    x = x_ref[...]
    h = jnp.dot(x.astype(BF16), wu_ref[...], preferred_element_type=F32)
    h_ext = jnp.concatenate([prev_ref[...], h], axis=0)
    prev_ref[...] = h[tm - CONV_HALO:, :]
    cw = cw_ref[...]
    conv = cw[CONV_W - 1:CONV_W, :] * h + cb_ref[...]
    for back in range(1, CONV_W):
        shifted = pltpu.roll(h_ext, back, axis=0)[CONV_HALO:, :]
        conv = conv + cw[CONV_W - 1 - back:CONV_W - back, :] * shifted
    up, gate = conv[:, :D_FF], conv[:, D_FF:]
    act = (gate / (1.0 + jnp.exp(-gate)) * up).astype(BF16)
    y = jnp.dot(act, wd_ref[...], preferred_element_type=F32)
    out_ref[...] = _layer_norm(ALPHA * x + y, g_ref[...], b_ref[...])


def _ffn(x, w_up, conv_w, conv_b, w_down, g, b, seq):
    n, d = x.shape
    tm = FFN_ROW_TILE
    resident = dict(pipeline_mode=pl.Buffered(1))
    return pl.pallas_call(
        functools.partial(_ffn_kernel, tiles_per_seq=seq // tm),
        grid=(n // tm,),
        in_specs=[pl.BlockSpec((tm, d), lambda i: (i, 0)),
                  pl.BlockSpec((d, 2 * D_FF), lambda i: (0, 0), **resident),
                  pl.BlockSpec((CONV_W, 2 * D_FF), lambda i: (0, 0)),
                  pl.BlockSpec((1, 2 * D_FF), lambda i: (0, 0)),
                  pl.BlockSpec((D_FF, d), lambda i: (0, 0), **resident),
                  pl.BlockSpec((1, d), lambda i: (0, 0)),
                  pl.BlockSpec((1, d), lambda i: (0, 0))],
        out_specs=pl.BlockSpec((tm, d), lambda i: (i, 0)),
        out_shape=jax.ShapeDtypeStruct((n, d), F32),
        scratch_shapes=[pltpu.VMEM((CONV_HALO, 2 * D_FF), F32)],
        compiler_params=_params(1),
        name="conv_ffn",
    )(x, w_up, conv_w, conv_b, w_down, g, b)


def kernel(x, sb_w_qkv, sb_w_o, gla_w_in, gla_w_a1, gla_w_a2, gla_b_a, gla_norm_g, gla_w_o,
           ffn_w_up, ffn_conv_w, ffn_conv_b, ffn_w_down, ln_mix_g, ln_mix_b, ln_ffn_g, ln_ffn_b):
    batch, seq, d = x.shape
    h = x.reshape(batch * seq, d)
    q_scale = jnp.concatenate([jnp.full((D_MODEL,), SB_HEAD_DIM ** -0.5, F32), jnp.ones((2 * D_MODEL,), F32)])
    row = lambda a: a.reshape(1, -1)
    for i in range(DEPTH):
        j = i // N_MIXERS
        if i % N_MIXERS == 0:
            qkv = _proj(h, (sb_w_qkv[j] * q_scale).astype(BF16))
            mixed = _sb_attention(qkv, batch, seq)
            w_o = sb_w_o[j]
        else:
            pad = LANES - GLA_GATE_RANK
            w_a1 = jnp.pad(gla_w_a1[j], ((0, 0), (0, pad))).astype(BF16)
            w_a2 = jnp.pad(gla_w_a2[j], ((0, pad), (0, 0))).astype(BF16)
            proj, gate = _gla_proj(h, gla_w_in[j].astype(BF16), w_a1, w_a2, row(gla_b_a[j]))
            mixed = _gla(proj, gate, row(gla_norm_g[j]), batch, seq)
            w_o = gla_w_o[j]
        h = _oproj_ln(mixed, h, w_o.astype(BF16), row(ln_mix_g[i]), row(ln_mix_b[i]))
        h = _ffn(h, ffn_w_up[i].astype(BF16), ffn_conv_w[i], row(ffn_conv_b[i]),
                 ffn_w_down[i].astype(BF16), row(ln_ffn_g[i]), row(ln_ffn_b[i]), seq)
    return h.reshape(batch, seq, d)
```

```python
import functools

import jax
import jax.numpy as jnp
from jax import lax
from jax.experimental import pallas as pl
from jax.experimental.pallas import tpu as pltpu

F32 = jnp.float32
BF16 = jnp.bfloat16

D_MODEL = 1024
DEPTH = 4
N_MIXERS = 2
ALPHA = (2.0 * DEPTH) ** 0.25

SB_HEADS = 16
SB_HEAD_DIM = D_MODEL // SB_HEADS

GLA_HEADS = 4
GLA_KEY_DIM = D_MODEL // 2
GLA_VALUE_DIM = D_MODEL
GLA_DK = GLA_KEY_DIM // GLA_HEADS
GLA_DV = GLA_VALUE_DIM // GLA_HEADS
GLA_GATE_RANK = 16
GLA_GATE_TAU = 16.0
GLA_CHUNK = 64
GLA_IN_DIM = 2 * GLA_KEY_DIM + 2 * GLA_VALUE_DIM

D_FF = ((8 * D_MODEL // 3 + 127) // 128) * 128
CONV_W = 3
LN_EPS = 1e-5

LANES = 128
MXU_DIM = 256
VMEM_LIMIT_BYTES = 56 * 1024 * 1024

ROW_TILE = 512
LOG2_E = 1.4426950408889634
SB_MAX_LOG2 = 126.0
SB_MASKED_LOG2 = -1e30
SB_BLOCK_Q = 512
SB_BLOCK_K = 256
GLA_TOKENS = 512
FFN_ROW_TILE = 256
CONV_HALO = 8


def _params(n_axes):
    return pltpu.CompilerParams(dimension_semantics=("arbitrary",) * n_axes,
                                vmem_limit_bytes=VMEM_LIMIT_BYTES)


def _layer_norm(r, g, b):
    mu = jnp.mean(r, axis=-1, keepdims=True)
    d = r - mu
    var = jnp.mean(d * d, axis=-1, keepdims=True)
    return d * lax.rsqrt(var + LN_EPS) * g + b


def _proj_kernel(x_ref, w_ref, o_ref):
    xb = x_ref[...].astype(BF16)
    n_out = o_ref.shape[1]
    for c in range(n_out // ROW_TILE):
        cs = slice(c * ROW_TILE, (c + 1) * ROW_TILE)
        o_ref[:, cs] = jnp.dot(xb, w_ref[:, cs], preferred_element_type=F32).astype(o_ref.dtype)


def _proj(x, w):
    n, k = x.shape
    m = w.shape[1]
    return pl.pallas_call(
        _proj_kernel,
        grid=(n // ROW_TILE,),
        in_specs=[pl.BlockSpec((ROW_TILE, k), lambda i: (i, 0)),
                  pl.BlockSpec((k, m), lambda i: (0, 0))],
        out_specs=pl.BlockSpec((ROW_TILE, m), lambda i: (i, 0)),
        out_shape=jax.ShapeDtypeStruct((n, m), BF16),
        compiler_params=_params(1),
        name="proj",
    )(x, w)


def _gla_proj_kernel(x_ref, w_ref, wa1_ref, wa2_ref, ba_ref, o_ref, g_ref):
    xb = x_ref[...].astype(BF16)
    n_out = o_ref.shape[1]
    for c in range(n_out // ROW_TILE):
        cs = slice(c * ROW_TILE, (c + 1) * ROW_TILE)
        o_ref[:, cs] = jnp.dot(xb, w_ref[:, cs], preferred_element_type=F32).astype(o_ref.dtype)
    low = jnp.dot(xb, wa1_ref[...], preferred_element_type=F32).astype(BF16)
    pre = jnp.dot(low, wa2_ref[...], preferred_element_type=F32) + ba_ref[...]
    g_ref[...] = (jnp.minimum(pre, 0.0) - jnp.log(1.0 + jnp.exp(-jnp.abs(pre)))) * (1.0 / GLA_GATE_TAU)


def _gla_proj(x, w_in, w_a1, w_a2, b_a):
    n, k = x.shape
    m = w_in.shape[1]
    rank = w_a1.shape[1]
    return pl.pallas_call(
        _gla_proj_kernel,
        grid=(n // ROW_TILE,),
        in_specs=[pl.BlockSpec((ROW_TILE, k), lambda i: (i, 0)),
                  pl.BlockSpec((k, m), lambda i: (0, 0)),
                  pl.BlockSpec((k, rank), lambda i: (0, 0)),
                  pl.BlockSpec((rank, GLA_KEY_DIM), lambda i: (0, 0)),
                  pl.BlockSpec((1, GLA_KEY_DIM), lambda i: (0, 0))],
        out_specs=[pl.BlockSpec((ROW_TILE, m), lambda i: (i, 0)),
                   pl.BlockSpec((ROW_TILE, GLA_KEY_DIM), lambda i: (i, 0))],
        out_shape=[jax.ShapeDtypeStruct((n, m), BF16),
                   jax.ShapeDtypeStruct((n, GLA_KEY_DIM), F32)],
        compiler_params=_params(1),
        name="gla_proj",
    )(x, w_in, w_a1, w_a2, b_a)


def _oproj_ln_kernel(o_ref, x_ref, w_ref, g_ref, b_ref, out_ref):
    y = jnp.dot(o_ref[...], w_ref[...], preferred_element_type=F32)
    out_ref[...] = _layer_norm(ALPHA * x_ref[...] + y, g_ref[...], b_ref[...])


def _oproj_ln(o, x, w, g, b):
    n, k = o.shape
    d = w.shape[1]
    return pl.pallas_call(
        _oproj_ln_kernel,
        grid=(n // ROW_TILE,),
        in_specs=[pl.BlockSpec((ROW_TILE, k), lambda i: (i, 0)),
                  pl.BlockSpec((ROW_TILE, d), lambda i: (i, 0)),
                  pl.BlockSpec((k, d), lambda i: (0, 0)),
                  pl.BlockSpec((1, d), lambda i: (0, 0)),
                  pl.BlockSpec((1, d), lambda i: (0, 0))],
        out_specs=pl.BlockSpec((ROW_TILE, d), lambda i: (i, 0)),
        out_shape=jax.ShapeDtypeStruct((n, d), F32),
        compiler_params=_params(1),
        name="oproj_ln",
    )(o, x, w, g, b)


def _sb_attn_kernel(q_ref, k_ref, v_ref, w_ref, o_ref, acc_ref, carry_ref, qs_ref, z_ref, loga_ref):
    qi = pl.program_id(2)
    tq, tk = SB_BLOCK_Q, SB_BLOCK_K
    lane = lax.broadcasted_iota(jnp.int32, (1, LANES), 1)
    head_lanes = [lane < SB_HEAD_DIM, lane >= SB_HEAD_DIM]
    q = q_ref[...]
    qs_ref[...] = jnp.concatenate([jnp.where(m, q, jnp.zeros_like(q)) for m in head_lanes], axis=0)
    row = lax.broadcasted_iota(jnp.int32, (2 * tq, tk), 0) & (tq - 1)
    col = lax.broadcasted_iota(jnp.int32, (2 * tq, tk), 1)

    acc_ref[...] = jnp.zeros_like(acc_ref)
    carry_ref[...] = jnp.zeros_like(carry_ref)

    assert tq == 2 * tk
    first = 2 * qi

    def block_of(n):
        return jnp.maximum(first + 1 - n, 0)

    def scores(n, slot):
        start = pl.multiple_of(block_of(n) * tk, tk)
        z = lax.dot_general(qs_ref[...], k_ref[pl.ds(start, tk), :], (((1,), (1,)), ((), ())),
                            preferred_element_type=F32)
        z_ref[slot] = jnp.minimum(z, SB_MAX_LOG2)

    def log_weights(slot, diag_offset):
        z = z_ref[1 - slot]
        sp = jnp.log(1.0 + jnp.exp2(z)) * LOG2_E
        if diag_offset is not None:
            strictly_causal = col + diag_offset < row
            sp = jnp.where(strictly_causal, sp, 0.0)
        hi = lax.bitcast_convert_type(
            lax.bitcast_convert_type(sp, jnp.uint32) & jnp.uint32(0xFFFF0000), F32)
        lo = sp - hi
        split = jnp.concatenate([hi.astype(BF16), lo.astype(BF16)], axis=1)
        suffix = jnp.dot(split, w_ref[...], preferred_element_type=F32)
        carry = carry_ref[...]
        log_a = z + suffix + carry
        if diag_offset is not None:
            log_a = jnp.where(strictly_causal, log_a, SB_MASKED_LOG2)
        loga_ref[slot] = log_a
        carry_ref[...] = carry + suffix[:, 0:1]

    def weighted_values(n, slot):
        start = pl.multiple_of(block_of(n) * tk, tk)
        vs = v_ref[pl.ds(start, tk), :]
        a = jnp.exp2(loga_ref[1 - slot]).astype(BF16)
        a = jnp.concatenate([a[:tq], a[tq:]], axis=1)
        v2 = jnp.concatenate([jnp.where(m, vs, jnp.zeros_like(vs)) for m in head_lanes], axis=0)
        acc_ref[...] += jnp.dot(a, v2, preferred_element_type=F32)

    scores(0, 0)
    log_weights(1, tk)
    scores(1, 1)
    weighted_values(0, 0)
    log_weights(0, 0)
    scores(2, 0)

    def body(i, c):
        t = 3 + 2 * i
        weighted_values(t - 2, 1)
        log_weights(1, None)
        scores(t, 1)
        weighted_values(t - 1, 0)
        log_weights(0, None)
        scores(t + 1, 0)
        return c

    lax.fori_loop(0, qi, body, 0)
    weighted_values(first + 1, 1)
    o_ref[...] = acc_ref[...].astype(o_ref.dtype)


def _sb_attention(qkv, batch, seq):
    n = qkv.shape[0]
    assert SB_BLOCK_Q % SB_BLOCK_K == 0 and seq % SB_BLOCK_Q == 0
    n_q = seq // SB_BLOCK_Q
    pairs = D_MODEL // LANES
    jj = lax.broadcasted_iota(jnp.int32, (SB_BLOCK_K, SB_BLOCK_K), 0)
    ss = lax.broadcasted_iota(jnp.int32, (SB_BLOCK_K, SB_BLOCK_K), 1)
    tri = -(jj >= ss).astype(BF16)
    u = jnp.concatenate([tri, tri], axis=0)
    return pl.pallas_call(
        _sb_attn_kernel,
        grid=(batch, pairs, n_q),
        in_specs=[pl.BlockSpec((SB_BLOCK_Q, LANES), lambda b, p, i: (b * n_q + i, p)),
                  pl.BlockSpec((seq, LANES), lambda b, p, i: (b, pairs + p)),
                  pl.BlockSpec((seq, LANES), lambda b, p, i: (b, 2 * pairs + p)),
                  pl.BlockSpec((2 * SB_BLOCK_K, SB_BLOCK_K), lambda b, p, i: (0, 0))],
        out_specs=pl.BlockSpec((SB_BLOCK_Q, LANES), lambda b, p, i: (b * n_q + i, p)),
        out_shape=jax.ShapeDtypeStruct((n, D_MODEL), BF16),
        scratch_shapes=[pltpu.VMEM((SB_BLOCK_Q, LANES), F32),
                        pltpu.VMEM((2 * SB_BLOCK_Q, 1), F32),
                        pltpu.VMEM((2 * SB_BLOCK_Q, LANES), BF16),
                        pltpu.VMEM((2, 2 * SB_BLOCK_Q, SB_BLOCK_K), F32),
                        pltpu.VMEM((2, 2 * SB_BLOCK_Q, SB_BLOCK_K), F32)],
        compiler_params=_params(3),
        name="sb_attn",
    )(qkv, qkv, qkv, u)


def _gla_kernel(q_ref, k_ref, v_ref, r_ref, g_ref, ng_ref, lt_ref, o_ref, state_ref):
    si = pl.program_id(2)
    c_len = GLA_CHUNK

    @pl.when(si == 0)
    def _():
        state_ref[...] = jnp.zeros_like(state_ref)

    lt = lt_ref[...]
    row = lax.broadcasted_iota(jnp.int32, (c_len, c_len), 0)
    col = lax.broadcasted_iota(jnp.int32, (c_len, c_len), 1)
    causal = col <= row
    norm_g = ng_ref[...]
    nt = (((1,), (1,)), ((), ()))
    tn = (((0,), (0,)), ((), ()))
    for c in range(GLA_TOKENS // c_len):
        rows = slice(c * c_len, (c + 1) * c_len)
        q = q_ref[rows, :].astype(F32) * (GLA_DK ** -0.5)
        k = k_ref[rows, :].astype(F32)
        v = v_ref[rows, :]
        g = g_ref[rows, :]
        g_hi = g.astype(BF16)
        g_lo = (g - g_hi.astype(F32)).astype(BF16)
        b = (jnp.dot(lt, g_hi, preferred_element_type=F32)
             + jnp.dot(lt, g_lo, preferred_element_type=F32))
        b_last = b[c_len - 1:c_len, :]
        q_dec = (q * jnp.exp(b)).astype(BF16)
        k_inv = (k * jnp.exp(-b)).astype(BF16)
        k_dec = (k * jnp.exp(b_last - b)).astype(BF16)
        attn = lax.dot_general(q_dec, k_inv, nt, preferred_element_type=F32)
        attn = jnp.where(causal, attn, 0.0).astype(BF16)
        state = state_ref[...]
        o = (jnp.dot(attn, v, preferred_element_type=F32)
             + lax.dot_general(q_dec, state.astype(BF16), nt, preferred_element_type=F32))
        kv = lax.dot_general(v, k_dec, tn, preferred_element_type=F32)
        state_ref[...] = state * jnp.exp(b_last) + kv
        o = o * lax.rsqrt(jnp.mean(o * o, axis=-1, keepdims=True) + LN_EPS) * norm_g
        r = r_ref[rows, :].astype(F32)
        o_ref[rows, :] = (o * (r / (1.0 + jnp.exp(-r)))).astype(o_ref.dtype)


def _gla(proj, g, norm_g, batch, seq):
    n = proj.shape[0]
    steps = seq // GLA_TOKENS
    t = GLA_TOKENS
    k_off = GLA_KEY_DIM // GLA_DK
    v_off = 2 * GLA_KEY_DIM // GLA_DV
    r_off = (2 * GLA_KEY_DIM + GLA_VALUE_DIM) // GLA_DV
    ii = lax.broadcasted_iota(jnp.int32, (GLA_CHUNK, GLA_CHUNK), 0)
    jj = lax.broadcasted_iota(jnp.int32, (GLA_CHUNK, GLA_CHUNK), 1)
    lt = (jj <= ii).astype(BF16)
    return pl.pallas_call(
        _gla_kernel,
        grid=(batch, GLA_HEADS, steps),
        in_specs=[pl.BlockSpec((t, GLA_DK), lambda b, h, s: (b * steps + s, h)),
                  pl.BlockSpec((t, GLA_DK), lambda b, h, s: (b * steps + s, k_off + h)),
                  pl.BlockSpec((t, GLA_DV), lambda b, h, s: (b * steps + s, v_off + h)),
                  pl.BlockSpec((t, GLA_DV), lambda b, h, s: (b * steps + s, r_off + h)),
                  pl.BlockSpec((t, GLA_DK), lambda b, h, s: (b * steps + s, h)),
                  pl.BlockSpec((1, GLA_DV), lambda b, h, s: (0, 0)),
                  pl.BlockSpec((GLA_CHUNK, GLA_CHUNK), lambda b, h, s: (0, 0))],
        out_specs=pl.BlockSpec((t, GLA_DV), lambda b, h, s: (b * steps + s, h)),
        out_shape=jax.ShapeDtypeStruct((n, GLA_VALUE_DIM), BF16),
        scratch_shapes=[pltpu.VMEM((GLA_DV, GLA_DK), F32)],
        compiler_params=_params(3),
        name="gla",
    )(proj, proj, proj, proj, g, norm_g, lt)


def _ffn_kernel(x_ref, wu_ref, cw_ref, cb_ref, wd_ref, g_ref, b_ref, out_ref, prev_ref, *, tiles_per_seq):
    i = pl.program_id(0)
    tm = FFN_ROW_TILE

    @pl.when(i % tiles_per_seq == 0)
    def _():
        prev_ref[...] = jnp.zeros_like(prev_ref)

    x = x_ref[...]
    h = jnp.dot(x.astype(BF16), wu_ref[...], preferred_element_type=F32)
    h_ext = jnp.concatenate([prev_ref[...], h], axis=0)
    prev_ref[...] = h[tm - CONV_HALO:, :]
    cw = cw_ref[...]
    conv = cw[CONV_W - 1:CONV_W, :] * h + cb_ref[...]
    for back in range(1, CONV_W):
        shifted = pltpu.roll(h_ext, back, axis=0)[CONV_HALO:, :]
        conv = conv + cw[CONV_W - 1 - back:CONV_W - back, :] * shifted
    up, gate = conv[:, :D_FF], conv[:, D_FF:]
    act = (gate / (1.0 + jnp.exp(-gate)) * up).astype(BF16)
    y = jnp.dot(act, wd_ref[...], preferred_element_type=F32)
    out_ref[...] = _layer_norm(ALPHA * x + y, g_ref[...], b_ref[...])


def _ffn(x, w_up, conv_w, conv_b, w_down, g, b, seq):
    n, d = x.shape
    tm = FFN_ROW_TILE
    resident = dict(pipeline_mode=pl.Buffered(1))
    return pl.pallas_call(
        functools.partial(_ffn_kernel, tiles_per_seq=seq // tm),
        grid=(n // tm,),
        in_specs=[pl.BlockSpec((tm, d), lambda i: (i, 0)),
                  pl.BlockSpec((d, 2 * D_FF), lambda i: (0, 0), **resident),
                  pl.BlockSpec((CONV_W, 2 * D_FF), lambda i: (0, 0)),
                  pl.BlockSpec((1, 2 * D_FF), lambda i: (0, 0)),
                  pl.BlockSpec((D_FF, d), lambda i: (0, 0), **resident),
                  pl.BlockSpec((1, d), lambda i: (0, 0)),
                  pl.BlockSpec((1, d), lambda i: (0, 0))],
        out_specs=pl.BlockSpec((tm, d), lambda i: (i, 0)),
        out_shape=jax.ShapeDtypeStruct((n, d), F32),
        scratch_shapes=[pltpu.VMEM((CONV_HALO, 2 * D_FF), F32)],
        compiler_params=_params(1),
        name="conv_ffn",
    )(x, w_up, conv_w, conv_b, w_down, g, b)


def kernel(x, sb_w_qkv, sb_w_o, gla_w_in, gla_w_a1, gla_w_a2, gla_b_a, gla_norm_g, gla_w_o,
           ffn_w_up, ffn_conv_w, ffn_conv_b, ffn_w_down, ln_mix_g, ln_mix_b, ln_ffn_g, ln_ffn_b):
    batch, seq, d = x.shape
    h = x.reshape(batch * seq, d)
    q_scale = jnp.concatenate([jnp.full((D_MODEL,), LOG2_E * SB_HEAD_DIM ** -0.5, F32),
                               jnp.ones((2 * D_MODEL,), F32)])
    row = lambda a: a.reshape(1, -1)
    for i in range(DEPTH):
        j = i // N_MIXERS
        if i % N_MIXERS == 0:
            qkv = _proj(h, (sb_w_qkv[j] * q_scale).astype(BF16))
            mixed = _sb_attention(qkv, batch, seq)
            w_o = sb_w_o[j]
        else:
            pad = LANES - GLA_GATE_RANK
            w_a1 = jnp.pad(gla_w_a1[j], ((0, 0), (0, pad))).astype(BF16)
            w_a2 = jnp.pad(gla_w_a2[j], ((0, pad), (0, 0))).astype(BF16)
            proj, gate = _gla_proj(h, gla_w_in[j].astype(BF16), w_a1, w_a2, row(gla_b_a[j]))
            mixed = _gla(proj, gate, row(gla_norm_g[j]), batch, seq)
            w_o = gla_w_o[j]
        h = _oproj_ln(mixed, h, w_o.astype(BF16), row(ln_mix_g[i]), row(ln_mix_b[i]))
        h = _ffn(h, ffn_w_up[i].astype(BF16), ffn_conv_w[i], row(ffn_conv_b[i]),
                 ffn_w_down[i].astype(BF16), row(ln_ffn_g[i]), row(ln_ffn_b[i]), seq)
    return h.reshape(batch, seq, d)
```

```python
import functools

import jax
import jax.numpy as jnp
from jax import lax
from jax.experimental import pallas as pl
from jax.experimental.pallas import tpu as pltpu

F32 = jnp.float32
BF16 = jnp.bfloat16

D_MODEL = 1024
DEPTH = 4
N_MIXERS = 2
ALPHA = (2.0 * DEPTH) ** 0.25

SB_HEADS = 16
SB_HEAD_DIM = D_MODEL // SB_HEADS

GLA_HEADS = 4
GLA_KEY_DIM = D_MODEL // 2
GLA_VALUE_DIM = D_MODEL
GLA_DK = GLA_KEY_DIM // GLA_HEADS
GLA_DV = GLA_VALUE_DIM // GLA_HEADS
GLA_GATE_RANK = 16
GLA_GATE_TAU = 16.0
GLA_CHUNK = 64
GLA_IN_DIM = 2 * GLA_KEY_DIM + 2 * GLA_VALUE_DIM

D_FF = ((8 * D_MODEL // 3 + 127) // 128) * 128
CONV_W = 3
LN_EPS = 1e-5

LANES = 128
MXU_DIM = 256
VMEM_LIMIT_BYTES = 56 * 1024 * 1024

ROW_TILE = 512
LOG2_E = 1.4426950408889634
SB_MAX_LOG2 = 126.0
SB_MASKED_LOG2 = -1e30
SB_BLOCK_Q = 512
SB_BLOCK_K = 256
GLA_TOKENS = 512
FFN_ROW_TILE = 256
CONV_HALO = 8


def _params(n_axes):
    return pltpu.CompilerParams(dimension_semantics=("arbitrary",) * n_axes,
                                vmem_limit_bytes=VMEM_LIMIT_BYTES)


def _layer_norm(r, g, b):
    mu = jnp.mean(r, axis=-1, keepdims=True)
    d = r - mu
    var = jnp.mean(d * d, axis=-1, keepdims=True)
    return d * lax.rsqrt(var + LN_EPS) * g + b


def _proj_kernel(x_ref, w_ref, o_ref):
    xb = x_ref[...].astype(BF16)
    n_out = o_ref.shape[1]
    for c in range(n_out // ROW_TILE):
        cs = slice(c * ROW_TILE, (c + 1) * ROW_TILE)
        o_ref[:, cs] = jnp.dot(xb, w_ref[:, cs], preferred_element_type=F32).astype(o_ref.dtype)


def _proj(x, w):
    n, k = x.shape
    m = w.shape[1]
    return pl.pallas_call(
        _proj_kernel,
        grid=(n // ROW_TILE,),
        in_specs=[pl.BlockSpec((ROW_TILE, k), lambda i: (i, 0)),
                  pl.BlockSpec((k, m), lambda i: (0, 0))],
        out_specs=pl.BlockSpec((ROW_TILE, m), lambda i: (i, 0)),
        out_shape=jax.ShapeDtypeStruct((n, m), BF16),
        compiler_params=_params(1),
        name="proj",
    )(x, w)


def _gla_proj_kernel(x_ref, w_ref, wa1_ref, wa2_ref, ba_ref, o_ref, g_ref):
    xb = x_ref[...].astype(BF16)
    n_out = o_ref.shape[1]
    for c in range(n_out // ROW_TILE):
        cs = slice(c * ROW_TILE, (c + 1) * ROW_TILE)
        o_ref[:, cs] = jnp.dot(xb, w_ref[:, cs], preferred_element_type=F32).astype(o_ref.dtype)
    low = jnp.dot(xb, wa1_ref[...], preferred_element_type=F32).astype(BF16)
    pre = jnp.dot(low, wa2_ref[...], preferred_element_type=F32) + ba_ref[...]
    g_ref[...] = (jnp.minimum(pre, 0.0) - jnp.log(1.0 + jnp.exp(-jnp.abs(pre)))) * (1.0 / GLA_GATE_TAU)


def _gla_proj(x, w_in, w_a1, w_a2, b_a):
    n, k = x.shape
    m = w_in.shape[1]
    rank = w_a1.shape[1]
    return pl.pallas_call(
        _gla_proj_kernel,
        grid=(n // ROW_TILE,),
        in_specs=[pl.BlockSpec((ROW_TILE, k), lambda i: (i, 0)),
                  pl.BlockSpec((k, m), lambda i: (0, 0)),
                  pl.BlockSpec((k, rank), lambda i: (0, 0)),
                  pl.BlockSpec((rank, GLA_KEY_DIM), lambda i: (0, 0)),
                  pl.BlockSpec((1, GLA_KEY_DIM), lambda i: (0, 0))],
        out_specs=[pl.BlockSpec((ROW_TILE, m), lambda i: (i, 0)),
                   pl.BlockSpec((ROW_TILE, GLA_KEY_DIM), lambda i: (i, 0))],
        out_shape=[jax.ShapeDtypeStruct((n, m), BF16),
                   jax.ShapeDtypeStruct((n, GLA_KEY_DIM), F32)],
        compiler_params=_params(1),
        name="gla_proj",
    )(x, w_in, w_a1, w_a2, b_a)


def _oproj_ln_kernel(o_ref, x_ref, w_ref, g_ref, b_ref, out_ref):
    y = jnp.dot(o_ref[...], w_ref[...], preferred_element_type=F32)
    out_ref[...] = _layer_norm(ALPHA * x_ref[...] + y, g_ref[...], b_ref[...])


def _oproj_ln(o, x, w, g, b):
    n, k = o.shape
    d = w.shape[1]
    return pl.pallas_call(
        _oproj_ln_kernel,
        grid=(n // ROW_TILE,),
        in_specs=[pl.BlockSpec((ROW_TILE, k), lambda i: (i, 0)),
                  pl.BlockSpec((ROW_TILE, d), lambda i: (i, 0)),
                  pl.BlockSpec((k, d), lambda i: (0, 0)),
                  pl.BlockSpec((1, d), lambda i: (0, 0)),
                  pl.BlockSpec((1, d), lambda i: (0, 0))],
        out_specs=pl.BlockSpec((ROW_TILE, d), lambda i: (i, 0)),
        out_shape=jax.ShapeDtypeStruct((n, d), F32),
        compiler_params=_params(1),
        name="oproj_ln",
    )(o, x, w, g, b)


def _sb_attn_kernel(q_ref, k_ref, v_ref, w_ref, o_ref, acc_ref, carry_ref, qs_ref, z_ref, loga_ref):
    qi = pl.program_id(2)
    tq, tk = SB_BLOCK_Q, SB_BLOCK_K
    lane = lax.broadcasted_iota(jnp.int32, (1, LANES), 1)
    head_lanes = [lane < SB_HEAD_DIM, lane >= SB_HEAD_DIM]
    q = q_ref[...]
    qs_ref[...] = jnp.concatenate([jnp.where(m, q, jnp.zeros_like(q)) for m in head_lanes], axis=0)
    row = lax.broadcasted_iota(jnp.int32, (2 * tq, tk), 0) & (tq - 1)
    col = lax.broadcasted_iota(jnp.int32, (2 * tq, tk), 1)

    acc_ref[...] = jnp.zeros_like(acc_ref)
    carry_ref[...] = jnp.zeros_like(carry_ref)

    assert tq == 2 * tk
    first = 2 * qi

    def block_of(n):
        return jnp.maximum(first + 1 - n, 0)

    def scores(n, slot):
        start = pl.multiple_of(block_of(n) * tk, tk)
        z = lax.dot_general(qs_ref[...], k_ref[pl.ds(start, tk), :], (((1,), (1,)), ((), ())),
                            preferred_element_type=F32)
        z_ref[slot] = jnp.minimum(z, SB_MAX_LOG2)

    def log_weights(slot, diag_offset):
        z = z_ref[1 - slot]
        sp = jnp.log(1.0 + jnp.exp2(z)) * LOG2_E
        if diag_offset is not None:
            strictly_causal = col + diag_offset < row
            sp = jnp.where(strictly_causal, sp, 0.0)
        suffix = jnp.dot(sp.astype(BF16), w_ref[...], preferred_element_type=F32)
        carry = carry_ref[...]
        log_a = z + suffix + carry
        if diag_offset is not None:
            log_a = jnp.where(strictly_causal, log_a, SB_MASKED_LOG2)
        loga_ref[slot] = log_a
        carry_ref[...] = carry + suffix[:, 0:1]

    def weighted_values(n, slot):
        start = pl.multiple_of(block_of(n) * tk, tk)
        vs = v_ref[pl.ds(start, tk), :]
        a = jnp.exp2(loga_ref[1 - slot]).astype(BF16)
        a = jnp.concatenate([a[:tq], a[tq:]], axis=1)
        v2 = jnp.concatenate([jnp.where(m, vs, jnp.zeros_like(vs)) for m in head_lanes], axis=0)
        acc_ref[...] += jnp.dot(a, v2, preferred_element_type=F32)

    scores(0, 0)
    log_weights(1, tk)
    scores(1, 1)
    weighted_values(0, 0)
    log_weights(0, 0)
    scores(2, 0)

    def step_pair(t):
        for s in (t, t + 1):
            slot = 1 if s is t else 0
            weighted_values(s - 2, slot)
            log_weights(slot, None)
            scores(s, slot)

    def body(i, c):
        t = 3 + 4 * i
        step_pair(t)
        step_pair(t + 2)
        return c

    lax.fori_loop(0, qi // 2, body, 0)

    @pl.when(qi % 2 == 1)
    def _():
        step_pair(2 * qi + 1)

    weighted_values(first + 1, 1)
    o_ref[...] = acc_ref[...].astype(o_ref.dtype)


def _sb_attention(qkv, batch, seq):
    n = qkv.shape[0]
    assert SB_BLOCK_Q % SB_BLOCK_K == 0 and seq % SB_BLOCK_Q == 0
    n_q = seq // SB_BLOCK_Q
    pairs = D_MODEL // LANES
    jj = lax.broadcasted_iota(jnp.int32, (SB_BLOCK_K, SB_BLOCK_K), 0)
    ss = lax.broadcasted_iota(jnp.int32, (SB_BLOCK_K, SB_BLOCK_K), 1)
    u = -(jj >= ss).astype(BF16)
    return pl.pallas_call(
        _sb_attn_kernel,
        grid=(batch, pairs, n_q),
        in_specs=[pl.BlockSpec((SB_BLOCK_Q, LANES), lambda b, p, i: (b * n_q + i, p)),
                  pl.BlockSpec((seq, LANES), lambda b, p, i: (b, pairs + p)),
                  pl.BlockSpec((seq, LANES), lambda b, p, i: (b, 2 * pairs + p)),
                  pl.BlockSpec((SB_BLOCK_K, SB_BLOCK_K), lambda b, p, i: (0, 0))],
        out_specs=pl.BlockSpec((SB_BLOCK_Q, LANES), lambda b, p, i: (b * n_q + i, p)),
        out_shape=jax.ShapeDtypeStruct((n, D_MODEL), BF16),
        scratch_shapes=[pltpu.VMEM((SB_BLOCK_Q, LANES), F32),
                        pltpu.VMEM((2 * SB_BLOCK_Q, 1), F32),
                        pltpu.VMEM((2 * SB_BLOCK_Q, LANES), BF16),
                        pltpu.VMEM((2, 2 * SB_BLOCK_Q, SB_BLOCK_K), F32),
                        pltpu.VMEM((2, 2 * SB_BLOCK_Q, SB_BLOCK_K), F32)],
        compiler_params=_params(3),
        name="sb_attn",
    )(qkv, qkv, qkv, u)


def _gla_kernel(q_ref, k_ref, v_ref, r_ref, g_ref, ng_ref, lt_ref, o_ref, state_ref):
    si = pl.program_id(2)
    c_len = GLA_CHUNK

    @pl.when(si == 0)
    def _():
        state_ref[...] = jnp.zeros_like(state_ref)

    lt = lt_ref[...]
    row = lax.broadcasted_iota(jnp.int32, (c_len, c_len), 0)
    col = lax.broadcasted_iota(jnp.int32, (c_len, c_len), 1)
    causal = col <= row
    norm_g = ng_ref[...]
    nt = (((1,), (1,)), ((), ()))
    tn = (((0,), (0,)), ((), ()))
    for c in range(GLA_TOKENS // c_len):
        rows = slice(c * c_len, (c + 1) * c_len)
        q = q_ref[rows, :].astype(F32) * (GLA_DK ** -0.5)
        k = k_ref[rows, :].astype(F32)
        v = v_ref[rows, :]
        g = g_ref[rows, :]
        g_hi = g.astype(BF16)
        g_lo = (g - g_hi.astype(F32)).astype(BF16)
        b = (jnp.dot(lt, g_hi, preferred_element_type=F32)
             + jnp.dot(lt, g_lo, preferred_element_type=F32))
        b_last = b[c_len - 1:c_len, :]
        q_dec = (q * jnp.exp(b)).astype(BF16)
        k_inv = (k * jnp.exp(-b)).astype(BF16)
        k_dec = (k * jnp.exp(b_last - b)).astype(BF16)
        attn = lax.dot_general(q_dec, k_inv, nt, preferred_element_type=F32)
        attn = jnp.where(causal, attn, 0.0).astype(BF16)
        state = state_ref[...]
        o = (jnp.dot(attn, v, preferred_element_type=F32)
             + lax.dot_general(q_dec, state.astype(BF16), nt, preferred_element_type=F32))
        kv = lax.dot_general(v, k_dec, tn, preferred_element_type=F32)
        state_ref[...] = state * jnp.exp(b_last) + kv
        o = o * lax.rsqrt(jnp.mean(o * o, axis=-1, keepdims=True) + LN_EPS) * norm_g
        r = r_ref[rows, :].astype(F32)
        o_ref[rows, :] = (o * (r / (1.0 + jnp.exp(-r)))).astype(o_ref.dtype)


def _gla(proj, g, norm_g, batch, seq):
    n = proj.shape[0]
    steps = seq // GLA_TOKENS
    t = GLA_TOKENS
    k_off = GLA_KEY_DIM // GLA_DK
    v_off = 2 * GLA_KEY_DIM // GLA_DV
    r_off = (2 * GLA_KEY_DIM + GLA_VALUE_DIM) // GLA_DV
    ii = lax.broadcasted_iota(jnp.int32, (GLA_CHUNK, GLA_CHUNK), 0)
    jj = lax.broadcasted_iota(jnp.int32, (GLA_CHUNK, GLA_CHUNK), 1)
    lt = (jj <= ii).astype(BF16)
    return pl.pallas_call(
        _gla_kernel,
        grid=(batch, GLA_HEADS, steps),
        in_specs=[pl.BlockSpec((t, GLA_DK), lambda b, h, s: (b * steps + s, h)),
                  pl.BlockSpec((t, GLA_DK), lambda b, h, s: (b * steps + s, k_off + h)),
                  pl.BlockSpec((t, GLA_DV), lambda b, h, s: (b * steps + s, v_off + h)),
                  pl.BlockSpec((t, GLA_DV), lambda b, h, s: (b * steps + s, r_off + h)),
                  pl.BlockSpec((t, GLA_DK), lambda b, h, s: (b * steps + s, h)),
                  pl.BlockSpec((1, GLA_DV), lambda b, h, s: (0, 0)),
                  pl.BlockSpec((GLA_CHUNK, GLA_CHUNK), lambda b, h, s: (0, 0))],
        out_specs=pl.BlockSpec((t, GLA_DV), lambda b, h, s: (b * steps + s, h)),
        out_shape=jax.ShapeDtypeStruct((n, GLA_VALUE_DIM), BF16),
        scratch_shapes=[pltpu.VMEM((GLA_DV, GLA_DK), F32)],
        compiler_params=_params(3),
        name="gla",
    )(proj, proj, proj, proj, g, norm_g, lt)


def _ffn_kernel(x_ref, wu_ref, cw_ref, cb_ref, wd_ref, g_ref, b_ref, out_ref, prev_ref, *, tiles_per_seq):
    i = pl.program_id(0)
    tm = FFN_ROW_TILE

    @pl.when(i % tiles_per_seq == 0)
    def _():
        prev_ref[...] = jnp.zeros_like(prev_ref)

    x = x_ref[...]
    h = jnp.dot(x.astype(BF16), wu_ref[...], preferred_element_type=F32)
    h_ext = jnp.concatenate([prev_ref[...], h], axis=0)
    prev_ref[...] = h[tm - CONV_HALO:, :]
    cw = cw_ref[...]
    conv = cw[CONV_W - 1:CONV_W, :] * h + cb_ref[...]
    for back in range(1, CONV_W):
        shifted = pltpu.roll(h_ext, back, axis=0)[CONV_HALO:, :]
        conv = conv + cw[CONV_W - 1 - back:CONV_W - back, :] * shifted
    up, gate = conv[:, :D_FF], conv[:, D_FF:]
    act = (gate / (1.0 + jnp.exp(-gate)) * up).astype(BF16)
    y = jnp.dot(act, wd_ref[...], preferred_element_type=F32)
    out_ref[...] = _layer_norm(ALPHA * x + y, g_ref[...], b_ref[...])


def _ffn(x, w_up, conv_w, conv_b, w_down, g, b, seq):
    n, d = x.shape
    tm = FFN_ROW_TILE
    resident = dict(pipeline_mode=pl.Buffered(1))
    return pl.pallas_call(
        functools.partial(_ffn_kernel, tiles_per_seq=seq // tm),
        grid=(n // tm,),
        in_specs=[pl.BlockSpec((tm, d), lambda i: (i, 0)),
                  pl.BlockSpec((d, 2 * D_FF), lambda i: (0, 0), **resident),
                  pl.BlockSpec((CONV_W, 2 * D_FF), lambda i: (0, 0)),
                  pl.BlockSpec((1, 2 * D_FF), lambda i: (0, 0)),
                  pl.BlockSpec((D_FF, d), lambda i: (0, 0), **resident),
                  pl.BlockSpec((1, d), lambda i: (0, 0)),
                  pl.BlockSpec((1, d), lambda i: (0, 0))],
        out_specs=pl.BlockSpec((tm, d), lambda i: (i, 0)),
        out_shape=jax.ShapeDtypeStruct((n, d), F32),
        scratch_shapes=[pltpu.VMEM((CONV_HALO, 2 * D_FF), F32)],
        compiler_params=_params(1),
        name="conv_ffn",
    )(x, w_up, conv_w, conv_b, w_down, g, b)


def kernel(x, sb_w_qkv, sb_w_o, gla_w_in, gla_w_a1, gla_w_a2, gla_b_a, gla_norm_g, gla_w_o,
           ffn_w_up, ffn_conv_w, ffn_conv_b, ffn_w_down, ln_mix_g, ln_mix_b, ln_ffn_g, ln_ffn_b):
    batch, seq, d = x.shape
    h = x.reshape(batch * seq, d)
    q_scale = jnp.concatenate([jnp.full((D_MODEL,), LOG2_E * SB_HEAD_DIM ** -0.5, F32),
                               jnp.ones((2 * D_MODEL,), F32)])
    row = lambda a: a.reshape(1, -1)
    for i in range(DEPTH):
        j = i // N_MIXERS
        if i % N_MIXERS == 0:
            qkv = _proj(h, (sb_w_qkv[j] * q_scale).astype(BF16))
            mixed = _sb_attention(qkv, batch, seq)
            w_o = sb_w_o[j]
        else:
            pad = LANES - GLA_GATE_RANK
            w_a1 = jnp.pad(gla_w_a1[j], ((0, 0), (0, pad))).astype(BF16)
            w_a2 = jnp.pad(gla_w_a2[j], ((0, pad), (0, 0))).astype(BF16)
            proj, gate = _gla_proj(h, gla_w_in[j].astype(BF16), w_a1, w_a2, row(gla_b_a[j]))
            mixed = _gla(proj, gate, row(gla_norm_g[j]), batch, seq)
            w_o = gla_w_o[j]
        h = _oproj_ln(mixed, h, w_o.astype(BF16), row(ln_mix_g[i]), row(ln_mix_b[i]))
        h = _ffn(h, ffn_w_up[i].astype(BF16), ffn_conv_w[i], row(ffn_conv_b[i]),
                 ffn_w_down[i].astype(BF16), row(ln_ffn_g[i]), row(ln_ffn_b[i]), seq)
    return h.reshape(batch, seq, d)
```
